```python
import jax, jax.numpy as jnp
from jax import lax
import numpy as np

D_MODEL = 2048
BATCH = 1
SEQ = 16384
DEPTH = 1

HEAD_DIM = 128
ROPE_THETA = 10000.0
Q_BLOCK = 128
LN_EPS = 1e-5
DEEPNORM_ALPHA = (2 * DEPTH) ** 0.25
DEEPNORM_BETA = (8 * DEPTH) ** -0.25
FFN_HIDDEN = 256 * ((8 * D_MODEL // 3 + 255) // 256)

NSA_HEADS = 8
NSA_KV_GROUPS = 2
NSA_HPG = NSA_HEADS // NSA_KV_GROUPS
CMP_BLOCK = 32
CMP_STRIDE = 16
CMP_HIDDEN = 2 * HEAD_DIM
SEL_BLOCK = 64
SEL_TOPK = 16
WIN = 512

DIL_PATTERNS = ((128, 1), (512, 4), (2048, 16))
N_DIL = len(DIL_PATTERNS)
DIL_HPG = 4
DIL_HEADS = N_DIL * DIL_HPG

NSA_Q_W = NSA_HEADS * HEAD_DIM
NSA_KV_W = NSA_KV_GROUPS * HEAD_DIM
NSA_GATE_W = NSA_HEADS * 3
DIL_W = DIL_HEADS * HEAD_DIM
IN_SIZES = (NSA_Q_W, NSA_KV_W, NSA_KV_W, NSA_KV_W, NSA_KV_W, NSA_KV_W, NSA_KV_W, NSA_GATE_W,
            DIL_W, DIL_W, DIL_W, D_MODEL, D_MODEL)
IN_WIDTH = sum(IN_SIZES)
BRANCH_A_W = NSA_HEADS * HEAD_DIM
BRANCH_B_W = DIL_HPG * HEAD_DIM

kernel_name = "hybrid_nsa_dilated_macaron_deepnorm"


def _layer_norm(x, g, b):
    xf = x.astype(jnp.float32)
    mu = jnp.mean(xf, -1, keepdims=True)
    var = jnp.mean(jnp.square(xf - mu), -1, keepdims=True)
    y = (xf - mu) * lax.rsqrt(var + LN_EPS) * g.astype(jnp.float32) + b.astype(jnp.float32)
    return y.astype(x.dtype)


def _swiglu(x, w_gate, w_up, w_down):
    return (jax.nn.silu(x @ w_gate) * (x @ w_up)) @ w_down


def _rope(x):
    s = x.shape[-2]
    inv = ROPE_THETA ** (-jnp.arange(0, HEAD_DIM, 2, dtype=jnp.float32) / HEAD_DIM)
    ang = jnp.arange(s, dtype=jnp.float32)[:, None] * inv[None, :]
    cos, sin = jnp.cos(ang), jnp.sin(ang)
    x1, x2 = jnp.split(x.astype(jnp.float32), 2, axis=-1)
    return jnp.concatenate([x1 * cos - x2 * sin, x2 * cos + x1 * sin], -1).astype(x.dtype)


def _masked_softmax(s, mask):
    s = jnp.where(mask, s, -jnp.inf)
    m = jnp.max(s, -1, keepdims=True)
    m = jnp.where(jnp.isfinite(m), m, 0.0)
    e = jnp.exp(s - m)
    d = jnp.sum(e, -1, keepdims=True)
    return e / jnp.where(d > 0, d, 1.0)


def _compress(t, pos, w1, w2):
    s = t.shape[2]
    n_cmp = (s - CMP_BLOCK) // CMP_STRIDE + 1
    idx = jnp.arange(n_cmp)[:, None] * CMP_STRIDE + jnp.arange(CMP_BLOCK)[None, :]
    blk = t[:, :, idx] + pos.astype(t.dtype)
    blk = blk.reshape(blk.shape[:3] + (CMP_BLOCK * HEAD_DIM,))
    return jax.nn.gelu(blk @ w1) @ w2


def _nsa(q, k_cmp, v_cmp, k_slc, v_slc, k_win, v_win, gate_logits,
         ck_pos, ck_w1, ck_w2, cv_pos, cv_w1, cv_w2):
    b, _, s, _ = q.shape
    G, HPG, dh = NSA_KV_GROUPS, NSA_HPG, HEAD_DIM
    scale = dh ** -0.5
    q_plain = q.reshape(b, G, HPG, s, dh)
    q_rot = _rope(q).reshape(b, G, HPG, s, dh)
    kc = _compress(k_cmp, ck_pos, ck_w1, ck_w2)
    vc = _compress(v_cmp, cv_pos, cv_w1, cv_w2)
    n_cmp = kc.shape[2]
    n_sel = s // SEL_BLOCK
    top = min(SEL_TOPK, n_sel)
    cmp_end = jnp.arange(n_cmp) * CMP_STRIDE + CMP_BLOCK - 1
    c0 = jnp.arange(n_cmp) * CMP_STRIDE
    s0 = jnp.arange(n_sel) * SEL_BLOCK
    overlap = jnp.clip(jnp.minimum(c0[:, None] + CMP_BLOCK, s0[None, :] + SEL_BLOCK)
                       - jnp.maximum(c0[:, None], s0[None, :]), 0)
    cmp_to_sel = overlap.astype(jnp.float32) / CMP_STRIDE
    ks = _rope(k_slc).reshape(b, G, n_sel, SEL_BLOCK, dh)
    vs = v_slc.reshape(b, G, n_sel, SEL_BLOCK, dh)
    pad = ((0, 0), (0, 0), (WIN, 0), (0, 0))
    kw = jnp.pad(_rope(k_win), pad)
    vw = jnp.pad(v_win, pad)
    gates = jax.nn.sigmoid(gate_logits.astype(jnp.float32)).reshape(b, s, G, HPG, 3).transpose(0, 2, 3, 1, 4)
    bi = jnp.arange(b)[:, None, None, None]
    gi = jnp.arange(G)[None, :, None, None]
    blk_id = jnp.arange(n_sel)

    def block(i):
        start = i * Q_BLOCK
        t = start + jnp.arange(Q_BLOCK)
        qc = lax.dynamic_slice_in_dim(q_plain, start, Q_BLOCK, axis=3)
        qr = lax.dynamic_slice_in_dim(q_rot, start, Q_BLOCK, axis=3)
        sc = jnp.einsum('bghqd,bgnd->bghqn', qc, kc).astype(jnp.float32) * scale
        pc = _masked_softmax(sc, cmp_end[None, :] <= t[:, None])
        o_cmp = jnp.einsum('bghqn,bgnd->bghqd', pc.astype(vc.dtype), vc)
        imp = jnp.einsum('bghqn,nj->bgqj', pc, cmp_to_sel)
        cur = t // SEL_BLOCK
        valid = blk_id[None, :] <= cur[:, None]
        forced = (blk_id[None, :] == 0) | (blk_id[None, :] == cur[:, None]) | (blk_id[None, :] == cur[:, None] - 1)
        imp = jnp.where(valid, jnp.where(forced, jnp.inf, imp), -jnp.inf)
        _, top_idx = lax.top_k(imp, top)
        k_sel = ks[bi, gi, top_idx].reshape(b, G, Q_BLOCK, top * SEL_BLOCK, dh)
        v_sel = vs[bi, gi, top_idx].reshape(b, G, Q_BLOCK, top * SEL_BLOCK, dh)
        kpos = (top_idx[..., None] * SEL_BLOCK + jnp.arange(SEL_BLOCK)).reshape(b, G, Q_BLOCK, top * SEL_BLOCK)
        sel_mask = kpos <= t[:, None]
        ssl = jnp.einsum('bghqd,bgqkd->bghqk', qr, k_sel).astype(jnp.float32) * scale
        ps = _masked_softmax(ssl, sel_mask[:, :, None])
        o_slc = jnp.einsum('bghqk,bgqkd->bghqd', ps.astype(v_sel.dtype), v_sel)
        kwb = lax.dynamic_slice_in_dim(kw, start, Q_BLOCK + WIN, axis=2)
        vwb = lax.dynamic_slice_in_dim(vw, start, Q_BLOCK + WIN, axis=2)
        wpos = start - WIN + jnp.arange(Q_BLOCK + WIN)
        wmask = (wpos[None, :] >= 0) & (wpos[None, :] <= t[:, None]) & (wpos[None, :] > t[:, None] - WIN)
        sw = jnp.einsum('bghqd,bgkd->bghqk', qr, kwb).astype(jnp.float32) * scale
        pw = _masked_softmax(sw, wmask)
        o_win = jnp.einsum('bghqk,bgkd->bghqd', pw.astype(vwb.dtype), vwb)
        g = lax.dynamic_slice_in_dim(gates, start, Q_BLOCK, axis=3)
        o = g[..., 0:1] * o_cmp + g[..., 1:2] * o_slc + g[..., 2:3] * o_win
        return o.astype(q.dtype)

    out = lax.map(block, jnp.arange(s // Q_BLOCK))
    return out.transpose(1, 0, 4, 2, 3, 5).reshape(b, s, NSA_HEADS * dh)


def _dilated(q, k, v):
    b, _, s, dh = q.shape
    scale = dh ** -0.5
    qg = q.reshape(b, N_DIL, DIL_HPG, s, dh)
    kg = k.reshape(b, N_DIL, DIL_HPG, s, dh)
    vg = v.reshape(b, N_DIL, DIL_HPG, s, dh)

    def block(i):
        start = i * Q_BLOCK
        t = start + jnp.arange(Q_BLOCK)
        qb = lax.dynamic_slice_in_dim(qg, start, Q_BLOCK, axis=3)
        outs, lses = [], []
        for gidx, (win, dil) in enumerate(DIL_PATTERNS):
            n_keys = win // dil + 1
            kpos = t[:, None] - dil * jnp.arange(n_keys)[None, :]
            idx = jnp.maximum(kpos, 0)
            kk = kg[:, gidx][:, :, idx]
            vv = vg[:, gidx][:, :, idx]
            sc = jnp.einsum('bhqd,bhqkd->bhqk', qb[:, gidx], kk).astype(jnp.float32) * scale
            sc = jnp.where(kpos >= 0, sc, -jnp.inf)
            lse = jax.nn.logsumexp(sc, axis=-1)
            p = jnp.exp(sc - lse[..., None])
            outs.append(jnp.einsum('bhqk,bhqkd->bhqd', p.astype(vv.dtype), vv).astype(jnp.float32))
            lses.append(lse)
        w = jax.nn.softmax(jnp.stack(lses), axis=0)
        o = jnp.sum(w[..., None] * jnp.stack(outs), axis=0)
        return o.astype(q.dtype)

    out = lax.map(block, jnp.arange(s // Q_BLOCK))
    return out.transpose(1, 0, 3, 2, 4).reshape(b, s, DIL_HPG * dh)


def _mixer(h, w_in, ck_pos, ck_w1, ck_w2, cv_pos, cv_w1, cv_w2, w_branch_a, w_branch_b, w_o):
    b, s, _ = h.shape
    proj = h @ w_in
    points, acc = [], 0
    for size in IN_SIZES[:-1]:
        acc += size
        points.append(acc)
    (q_a, kc, vc, ksl, vsl, kwn, vwn, g_nsa, q_b, k_b, v_b, g_a, g_b) = jnp.split(proj, points, axis=-1)

    def heads(t, n):
        return t.reshape(b, s, n, HEAD_DIM).transpose(0, 2, 1, 3)

    y_a = _nsa(heads(q_a, NSA_HEADS), heads(kc, NSA_KV_GROUPS), heads(vc, NSA_KV_GROUPS),
               heads(ksl, NSA_KV_GROUPS), heads(vsl, NSA_KV_GROUPS), heads(kwn, NSA_KV_GROUPS),
               heads(vwn, NSA_KV_GROUPS), g_nsa, ck_pos, ck_w1, ck_w2, cv_pos, cv_w1, cv_w2) @ w_branch_a
    y_b = _dilated(_rope(heads(q_b, DIL_HEADS)), _rope(heads(k_b, DIL_HEADS)), heads(v_b, DIL_HEADS)) @ w_branch_b
    merged = jax.nn.sigmoid(g_a) * y_a + jax.nn.sigmoid(g_b) * y_b
    return merged @ w_o


def setup_inputs(seed: int = 0) -> dict:
    key = jax.random.key(seed)
    ks = jax.random.split(key, 26)
    L = DEPTH

    def nrm(k, shape, scale):
        return jax.random.normal(k, shape, jnp.float32) * scale

    def gain(k):
        return 1.0 + nrm(k, (L, D_MODEL), 0.02)

    return {
        "x": nrm(ks[0], (BATCH, SEQ, D_MODEL), 1.0),
        "ffn1_w_gate": nrm(ks[1], (L, D_MODEL, FFN_HIDDEN), D_MODEL ** -0.5),
        "ffn1_w_up": nrm(ks[2], (L, D_MODEL, FFN_HIDDEN), D_MODEL ** -0.5),
        "ffn1_w_down": nrm(ks[3], (L, FFN_HIDDEN, D_MODEL), DEEPNORM_BETA * FFN_HIDDEN ** -0.5),
        "ln_ffn1_g": gain(ks[4]),
        "ln_ffn1_b": nrm(ks[5], (L, D_MODEL), 0.02),
        "w_in": nrm(ks[6], (L, D_MODEL, IN_WIDTH), D_MODEL ** -0.5),
        "cmp_k_pos": nrm(ks[7], (L, CMP_BLOCK, HEAD_DIM), 0.1),
        "cmp_k_w1": nrm(ks[8], (L, CMP_BLOCK * HEAD_DIM, CMP_HIDDEN), (CMP_BLOCK * HEAD_DIM) ** -0.5),
        "cmp_k_w2": nrm(ks[9], (L, CMP_HIDDEN, HEAD_DIM), CMP_HIDDEN ** -0.5),
        "cmp_v_pos": nrm(ks[10], (L, CMP_BLOCK, HEAD_DIM), 0.1),
        "cmp_v_w1": nrm(ks[11], (L, CMP_BLOCK * HEAD_DIM, CMP_HIDDEN), (CMP_BLOCK * HEAD_DIM) ** -0.5),
        "cmp_v_w2": nrm(ks[12], (L, CMP_HIDDEN, HEAD_DIM), CMP_HIDDEN ** -0.5),
        "w_branch_a": nrm(ks[13], (L, BRANCH_A_W, D_MODEL), BRANCH_A_W ** -0.5),
        "w_branch_b": nrm(ks[14], (L, BRANCH_B_W, D_MODEL), BRANCH_B_W ** -0.5),
        "w_o": nrm(ks[15], (L, D_MODEL, D_MODEL), DEEPNORM_BETA * D_MODEL ** -0.5),
        "ln_mix_g": gain(ks[16]),
        "ln_mix_b": nrm(ks[17], (L, D_MODEL), 0.02),
        "ffn2_w_gate": nrm(ks[18], (L, D_MODEL, FFN_HIDDEN), D_MODEL ** -0.5),
        "ffn2_w_up": nrm(ks[19], (L, D_MODEL, FFN_HIDDEN), D_MODEL ** -0.5),
        "ffn2_w_down": nrm(ks[20], (L, FFN_HIDDEN, D_MODEL), DEEPNORM_BETA * FFN_HIDDEN ** -0.5),
        "ln_ffn2_g": gain(ks[21]),
        "ln_ffn2_b": nrm(ks[22], (L, D_MODEL), 0.02),
    }


def reference(x, ffn1_w_gate, ffn1_w_up, ffn1_w_down, ln_ffn1_g, ln_ffn1_b, w_in,
              cmp_k_pos, cmp_k_w1, cmp_k_w2, cmp_v_pos, cmp_v_w1, cmp_v_w2,
              w_branch_a, w_branch_b, w_o, ln_mix_g, ln_mix_b,
              ffn2_w_gate, ffn2_w_up, ffn2_w_down, ln_ffn2_g, ln_ffn2_b):
    h = x
    for l in range(DEPTH):
        h = _layer_norm(DEEPNORM_ALPHA * h + 0.5 * _swiglu(h, ffn1_w_gate[l], ffn1_w_up[l], ffn1_w_down[l]),
                        ln_ffn1_g[l], ln_ffn1_b[l])
        mix = _mixer(h, w_in[l], cmp_k_pos[l], cmp_k_w1[l], cmp_k_w2[l], cmp_v_pos[l], cmp_v_w1[l], cmp_v_w2[l],
                     w_branch_a[l], w_branch_b[l], w_o[l])
        h = _layer_norm(DEEPNORM_ALPHA * h + mix, ln_mix_g[l], ln_mix_b[l])
        h = _layer_norm(DEEPNORM_ALPHA * h + 0.5 * _swiglu(h, ffn2_w_gate[l], ffn2_w_up[l], ffn2_w_down[l]),
                        ln_ffn2_g[l], ln_ffn2_b[l])
    return h
```

```python
import functools

import jax
import jax.numpy as jnp
from jax.experimental import pallas as pl
from jax.experimental.pallas import tpu as pltpu

F32 = jnp.float32
BF16 = jnp.bfloat16

D_MODEL = 2048
DEPTH = 1
HEAD_DIM = 128
ROPE_THETA = 10000.0
Q_BLOCK = 128
LN_EPS = 1e-5
DEEPNORM_ALPHA = (2 * DEPTH) ** 0.25
FFN_HIDDEN = 5632
NSA_HEADS = 8
NSA_KV_GROUPS = 2
NSA_HPG = NSA_HEADS // NSA_KV_GROUPS
CMP_BLOCK = 32
CMP_STRIDE = 16
CMP_HIDDEN = 2 * HEAD_DIM
SEL_BLOCK = 64
SEL_TOPK = 16
WIN = 512
DIL_PATTERNS = ((128, 1), (512, 4), (2048, 16))
N_DIL = len(DIL_PATTERNS)
DIL_HPG = 4
DIL_HEADS = N_DIL * DIL_HPG
NSA_Q_W = NSA_HEADS * HEAD_DIM
NSA_KV_W = NSA_KV_GROUPS * HEAD_DIM
NSA_GATE_W = NSA_HEADS * 3
DIL_W = DIL_HEADS * HEAD_DIM
SCALE = HEAD_DIM ** -0.5

LANES = 128
VMEM_LIMIT = 56 * 1024 * 1024
MASK_NEG = -1e30
GATE_ROWS = 16
SEL_TK = 512
SEL_MASK_BLOCKS = LANES


def _cparams(*sem):
    return pltpu.CompilerParams(dimension_semantics=sem, vmem_limit_bytes=VMEM_LIMIT)


def _layer_norm(y, g, b):
    mu = jnp.mean(y, axis=-1, keepdims=True)
    yc = y - mu
    var = jnp.mean(yc * yc, axis=-1, keepdims=True)
    return yc * jax.lax.rsqrt(var + LN_EPS) * g + b


def _sigmoid(x):
    return 1.0 / (1.0 + jnp.exp(-x))


def _ffn_kernel(x_ref, wg_ref, wu_ref, wd_ref, g_ref, b_ref, o_ref, ob_ref, xb_ref, acc_ref):
    f = pl.program_id(1)

    @pl.when(f == 0)
    def _():
        xb_ref[...] = x_ref[...].astype(BF16)
        acc_ref[...] = jnp.zeros_like(acc_ref)

    xb = xb_ref[...]
    gate = jnp.dot(xb, wg_ref[...], preferred_element_type=F32)
    up = jnp.dot(xb, wu_ref[...], preferred_element_type=F32)
    act = (gate * _sigmoid(gate)) * up
    acc_ref[...] += jnp.dot(act.astype(BF16), wd_ref[...], preferred_element_type=F32)

    @pl.when(f == pl.num_programs(1) - 1)
    def _():
        y = DEEPNORM_ALPHA * x_ref[...] + 0.5 * acc_ref[...]
        out = _layer_norm(y, g_ref[...], b_ref[...])
        o_ref[...] = out
        ob_ref[...] = out.astype(BF16)


def _ffn_ln(x, wg, wu, wd, g, b, *, tm=512, tf=512):
    s = x.shape[0]
    tm = min(tm, s)
    grid = (s // tm, FFN_HIDDEN // tf)
    return pl.pallas_call(
        _ffn_kernel,
        grid=grid,
        in_specs=[
            pl.BlockSpec((tm, D_MODEL), lambda i, f: (i, 0)),
            pl.BlockSpec((D_MODEL, tf), lambda i, f: (0, f)),
            pl.BlockSpec((D_MODEL, tf), lambda i, f: (0, f)),
            pl.BlockSpec((tf, D_MODEL), lambda i, f: (f, 0)),
            pl.BlockSpec((1, D_MODEL), lambda i, f: (0, 0)),
            pl.BlockSpec((1, D_MODEL), lambda i, f: (0, 0)),
        ],
        out_specs=[
            pl.BlockSpec((tm, D_MODEL), lambda i, f: (i, 0)),
            pl.BlockSpec((tm, D_MODEL), lambda i, f: (i, 0)),
        ],
        out_shape=[
            jax.ShapeDtypeStruct((s, D_MODEL), F32),
            jax.ShapeDtypeStruct((s, D_MODEL), BF16),
        ],
        scratch_shapes=[pltpu.VMEM((tm, D_MODEL), BF16), pltpu.VMEM((tm, D_MODEL), F32)],
        compiler_params=_cparams("parallel", "arbitrary"),
        name="ffn_ln",
    )(x, wg, wu, wd, g, b)


def _rope_tables(s):
    inv = ROPE_THETA ** (-jnp.arange(0, HEAD_DIM, 2, dtype=F32) / HEAD_DIM)
    ang = jnp.arange(s, dtype=F32)[:, None] * inv[None, :]
    cos, sin = jnp.cos(ang), jnp.sin(ang)
    return jnp.concatenate([cos, cos], -1), jnp.concatenate([-sin, sin], -1)


def _proj_nat_kernel(*refs, rope, scale, aug, tm):
    if rope:
        h_ref, w_ref, cos_ref, sin_ref, o_ref = refs
    else:
        h_ref, w_ref, o_ref = refs
    h = h_ref[...].reshape(tm, D_MODEL)
    res = jnp.dot(h, w_ref[...], preferred_element_type=F32)
    nh = res.shape[1] // HEAD_DIM
    if rope:
        cos = cos_ref[...].reshape(tm, HEAD_DIM)
        sin = sin_ref[...].reshape(tm, HEAD_DIM)
    for j in range(nh):
        x = res[:, j * HEAD_DIM:(j + 1) * HEAD_DIM]
        if rope:
            x = x * cos + pltpu.roll(x, HEAD_DIM // 2, 1) * sin
        if scale != 1.0:
            x = x * scale
        if aug:
            row = pl.program_id(0) * tm + jax.lax.broadcasted_iota(jnp.int32, (tm, LANES), 0)
            col = jax.lax.broadcasted_iota(jnp.int32, (tm, LANES), 1)
            onehot = jnp.where(((row // SEL_BLOCK) % SEL_MASK_BLOCKS) == col, 1.0, 0.0)
            o_ref[j, :, :HEAD_DIM] = x.astype(o_ref.dtype)
            o_ref[j, :, HEAD_DIM:] = onehot.astype(o_ref.dtype)
        else:
            o_ref[j] = x.astype(o_ref.dtype)


def _proj_nat(h, w, *, rope=False, scale=1.0, aug=False, dil=1, tables=None, tm=1024):
    s = h.shape[0]
    n = w.shape[1]
    nh = n // HEAD_DIM
    tm = min(tm, s // dil)
    na = s // (dil * tm)
    nsub = tm // LANES
    hv = h.reshape(s // (dil * LANES), LANES, dil * D_MODEL)

    def row_map(i, j):
        return (i % na, 0, i // na)

    in_specs = [
        pl.BlockSpec((nsub, LANES, D_MODEL), row_map),
        pl.BlockSpec((D_MODEL, n), lambda i, j: (0, 0)),
    ]
    args = [hv, w]
    if rope:
        cos, sin = tables
        for t in (cos, sin):
            args.append(t.reshape(s // (dil * LANES), LANES, dil * HEAD_DIM))
            in_specs.append(pl.BlockSpec((nsub, LANES, HEAD_DIM), row_map))
    width = 2 * HEAD_DIM if aug else HEAD_DIM
    return pl.pallas_call(
        functools.partial(_proj_nat_kernel, rope=rope, scale=scale, aug=aug, tm=tm),
        grid=(s // tm, 1),
        in_specs=in_specs,
        out_specs=pl.BlockSpec((nh, tm, width), lambda i, j: (0, i, 0)),
        out_shape=jax.ShapeDtypeStruct((nh, s, width), BF16),
        compiler_params=_cparams("parallel", "arbitrary"),
        name="proj_nat",
    )(*args)


def _proj_t_kernel(*refs, mode):
    if mode == "qa":
        wt_ref, h_ref, cos_ref, sin_ref, op_ref, or_ref = refs
    else:
        wt_ref, h_ref, o_ref = refs
    res = jax.lax.dot_general(wt_ref[...], h_ref[...], (((1,), (1,)), ((), ())),
                              preferred_element_type=F32)
    if mode == "qa":
        cos = cos_ref[...]
        sin = sin_ref[...]
        for j in range(res.shape[0] // HEAD_DIM):
            x = res[j * HEAD_DIM:(j + 1) * HEAD_DIM, :]
            op_ref[j * HEAD_DIM:(j + 1) * HEAD_DIM, :] = (x * SCALE).astype(BF16)
            xr = x * cos + pltpu.roll(x, HEAD_DIM // 2, 0) * sin
            or_ref[j * HEAD_DIM:(j + 1) * HEAD_DIM, :] = (xr * SCALE).astype(BF16)
    elif mode == "sigmoid":
        o_ref[...] = _sigmoid(res).astype(o_ref.dtype)
    else:
        o_ref[...] = res.astype(o_ref.dtype)


def _proj_t(h, wt, *, mode, tables_t=None, out_dtype=BF16, tm=1024):
    s = h.shape[0]
    n = wt.shape[0]
    tm = min(tm, s)
    in_specs = [
        pl.BlockSpec((n, D_MODEL), lambda i: (0, 0)),
        pl.BlockSpec((tm, D_MODEL), lambda i: (i, 0)),
    ]
    args = [wt, h]
    out_spec = pl.BlockSpec((n, tm), lambda i: (0, i))
    out_shape = jax.ShapeDtypeStruct((n, s), out_dtype)
    if mode == "qa":
        args += list(tables_t)
        in_specs += [pl.BlockSpec((HEAD_DIM, tm), lambda i: (0, i))] * 2
        out_spec = [out_spec, out_spec]
        out_shape = [out_shape, out_shape]
    return pl.pallas_call(
        functools.partial(_proj_t_kernel, mode=mode),
        grid=(s // tm,),
        in_specs=in_specs,
        out_specs=out_spec,
        out_shape=out_shape,
        compiler_params=_cparams("parallel"),
        name="proj_t_" + mode,
    )(*args)


def _proj_sig_kernel(h_ref, w_ref, o_ref):
    res = jnp.dot(h_ref[...], w_ref[...], preferred_element_type=F32)
    o_ref[...] = _sigmoid(res).astype(o_ref.dtype)


def _proj_sig(h, w, *, tm=1024, tn=512):
    s = h.shape[0]
    n = w.shape[1]
    tm = min(tm, s)
    return pl.pallas_call(
        _proj_sig_kernel,
        grid=(s // tm, n // tn),
        in_specs=[
            pl.BlockSpec((tm, D_MODEL), lambda i, j: (i, 0)),
            pl.BlockSpec((D_MODEL, tn), lambda i, j: (0, j)),
        ],
        out_specs=pl.BlockSpec((tm, tn), lambda i, j: (i, j)),
        out_shape=jax.ShapeDtypeStruct((s, n), BF16),
        compiler_params=_cparams("parallel", "arbitrary"),
        name="proj_sig",
    )(h, w)


def _gelu_tanh(x):
    return 0.5 * x * (1.0 + jnp.tanh(0.7978845608028654 * (x + 0.044715 * (x * x * x))))


def _compress_kernel(t_ref, pos_ref, w1_ref, w2_ref, o_ref, ot_ref, *, n_cmp):
    t2 = t_ref[0]
    nrow = t2.shape[0]
    a = jnp.dot(t2, w1_ref[0, 0], preferred_element_type=F32)
    b = jnp.dot(t2, w1_ref[0, 1], preferred_element_type=F32)
    pos = jnp.broadcast_to(pos_ref[0], (8, CMP_BLOCK * HEAD_DIM)).astype(BF16)
    w1 = w1_ref[0]
    posb = (jnp.dot(pos[:, :CMP_STRIDE * HEAD_DIM], w1[0], preferred_element_type=F32)
            + jnp.dot(pos[:, CMP_STRIDE * HEAD_DIM:], w1[1], preferred_element_type=F32))[0:1]
    pre = a + pltpu.roll(b, nrow - 1, 0) + posb
    hid = _gelu_tanh(pre)
    out = jnp.dot(hid.astype(BF16), w2_ref[0], preferred_element_type=F32)
    row = jax.lax.broadcasted_iota(jnp.int32, out.shape, 0)
    out = jnp.where(row < n_cmp, out, 0.0)
    o_ref[0] = out.astype(BF16)
    ot_ref[0] = out.T.astype(BF16)


def _compress(tok, pos, w1, w2):
    s = tok.shape[1]
    nrow = s // CMP_STRIDE
    n_cmp = (s - CMP_BLOCK) // CMP_STRIDE + 1
    t2 = tok.reshape(4, nrow, CMP_STRIDE * HEAD_DIM)
    kv = lambda j: j // NSA_KV_GROUPS
    return pl.pallas_call(
        functools.partial(_compress_kernel, n_cmp=n_cmp),
        grid=(4,),
        in_specs=[
            pl.BlockSpec((1, nrow, CMP_STRIDE * HEAD_DIM), lambda j: (j, 0, 0)),
            pl.BlockSpec((1, 1, CMP_BLOCK * HEAD_DIM), lambda j: (kv(j), 0, 0)),
            pl.BlockSpec((1, 2, CMP_STRIDE * HEAD_DIM, CMP_HIDDEN), lambda j: (kv(j), 0, 0, 0)),
            pl.BlockSpec((1, CMP_HIDDEN, HEAD_DIM), lambda j: (kv(j), 0, 0)),
        ],
        out_specs=[
            pl.BlockSpec((1, nrow, HEAD_DIM), lambda j: (j, 0, 0)),
            pl.BlockSpec((1, HEAD_DIM, nrow), lambda j: (j, 0, 0)),
        ],
        out_shape=[
            jax.ShapeDtypeStruct((4, nrow, HEAD_DIM), BF16),
            jax.ShapeDtypeStruct((4, HEAD_DIM, nrow), BF16),
        ],
        compiler_params=_cparams("parallel"),
        name="compress",
    )(t2, pos, w1, w2)


def _heads_to_lanes(x):
    return jnp.concatenate([x[h * HEAD_DIM:(h + 1) * HEAD_DIM, :] for h in range(NSA_HPG)], axis=1)


def _store_heads(o_ref, x):
    q = x.shape[1] // NSA_HPG
    for h in range(NSA_HPG):
        o_ref[h * HEAD_DIM:(h + 1) * HEAD_DIM, :] = x[:, h * q:(h + 1) * q].astype(o_ref.dtype)


def _cmp_topk_kernel(q_ref, kc_ref, vct_ref, ct_ref, o_ref, mb_ref, *, n_sel):
    i = pl.program_id(1)
    q4 = _heads_to_lanes(q_ref[...])
    nq = NSA_HPG * Q_BLOCK
    s_t = jnp.dot(kc_ref[0], q4, preferred_element_type=F32)
    nc = s_t.shape[0]
    n_idx = jax.lax.broadcasted_iota(jnp.int32, (nc, nq), 0)
    t_idx = i * Q_BLOCK + (jax.lax.broadcasted_iota(jnp.int32, (nc, nq), 1) % Q_BLOCK)
    valid = n_idx * CMP_STRIDE + (CMP_BLOCK - 1) <= t_idx
    s_t = jnp.where(valid, s_t, -jnp.inf)
    m = jnp.max(s_t, axis=0, keepdims=True)
    m = jnp.where(m > -jnp.inf, m, 0.0)
    e = jnp.exp(s_t - m)
    d = jnp.sum(e, axis=0, keepdims=True)
    p = e / jnp.where(d > 0, d, 1.0)
    o_t = jnp.dot(vct_ref[0], p.astype(BF16), preferred_element_type=F32)
    _store_heads(o_ref, o_t)

    psum = p[:, 0:Q_BLOCK]
    for h in range(1, NSA_HPG):
        psum = psum + p[:, h * Q_BLOCK:(h + 1) * Q_BLOCK]
    hi = psum.astype(BF16)
    r1 = psum - hi.astype(F32)
    mid = r1.astype(BF16)
    lo = (r1 - mid.astype(F32)).astype(BF16)
    ct = ct_ref[...]
    imp = (jnp.dot(ct, hi, preferred_element_type=F32) + jnp.dot(ct, mid, preferred_element_type=F32)
           + jnp.dot(ct, lo, preferred_element_type=F32))

    blk = jax.lax.broadcasted_iota(jnp.int32, (n_sel, Q_BLOCK), 0)
    t = i * Q_BLOCK + jax.lax.broadcasted_iota(jnp.int32, (n_sel, Q_BLOCK), 1)
    cur = t // SEL_BLOCK
    forced = (blk == 0) | (blk == cur) | (blk == cur - 1)
    v = jnp.where(blk <= cur, jnp.where(forced, jnp.inf, imp), -jnp.inf)
    sel = jnp.zeros((n_sel, Q_BLOCK), jnp.int32)
    for _ in range(min(SEL_TOPK, n_sel)):
        mx = jnp.max(v, axis=0, keepdims=True)
        first = jnp.min(jnp.where(v == mx, blk, n_sel), axis=0, keepdims=True)
        pick = (blk == first) & (mx > -jnp.inf)
        sel = jnp.where(pick, 1, sel)
        v = jnp.where(blk == first, -jnp.inf, v)
    mb_ref[0] = jnp.where(sel > 0, 0.0, MASK_NEG).astype(BF16)


def _cmp_topk(qp_t, cmp_nat, cmp_t, ct):
    s = qp_t.shape[1]
    nc = cmp_nat.shape[1]
    n_sel = s // SEL_BLOCK
    g4 = NSA_HPG * HEAD_DIM
    return pl.pallas_call(
        functools.partial(_cmp_topk_kernel, n_sel=n_sel),
        grid=(NSA_KV_GROUPS, s // Q_BLOCK),
        in_specs=[
            pl.BlockSpec((g4, Q_BLOCK), lambda g, i: (g, i)),
            pl.BlockSpec((1, nc, HEAD_DIM), lambda g, i: (g, 0, 0)),
            pl.BlockSpec((1, HEAD_DIM, nc), lambda g, i: (NSA_KV_GROUPS + g, 0, 0)),
            pl.BlockSpec((n_sel, nc), lambda g, i: (0, 0)),
        ],
        out_specs=[
            pl.BlockSpec((g4, Q_BLOCK), lambda g, i: (g, i)),
            pl.BlockSpec((1, n_sel, Q_BLOCK), lambda g, i: (g, 0, i)),
        ],
        out_shape=[
            jax.ShapeDtypeStruct((NSA_HEADS * HEAD_DIM, s), BF16),
            jax.ShapeDtypeStruct((NSA_KV_GROUPS, n_sel, s), BF16),
        ],
        compiler_params=_cparams("parallel", "arbitrary"),
        name="cmp_topk",
    )(qp_t, cmp_nat, cmp_t, ct)


def _sel_attn_kernel(q_ref, mb_ref, k_ref, vt_ref, o_ref, qa_ref, m_ref, l_ref, acc_ref, *, n_sel):
    i = pl.program_id(1)
    start = i * Q_BLOCK
    nq = NSA_HPG * Q_BLOCK
    q4 = _heads_to_lanes(q_ref[...])
    mb = mb_ref[0]
    n_half = max(n_sel // SEL_MASK_BLOCKS, 1)
    for w in range(n_half):
        mbw = mb[w * SEL_MASK_BLOCKS:(w + 1) * SEL_MASK_BLOCKS]
        if mbw.shape[0] < SEL_MASK_BLOCKS:
            mbw = jnp.concatenate(
                [mbw, jnp.zeros((SEL_MASK_BLOCKS - mbw.shape[0], Q_BLOCK), BF16)], axis=0)
        qa_ref[w, :HEAD_DIM, :] = q4
        qa_ref[w, HEAD_DIM:, :] = jnp.concatenate([mbw] * NSA_HPG, axis=1)

    m_ref[...] = jnp.full(m_ref.shape, -jnp.inf, F32)
    l_ref[...] = jnp.zeros_like(l_ref)
    acc_ref[...] = jnp.zeros_like(acc_ref)

    def tile(kt, masked):
        k0 = pl.multiple_of(kt * SEL_TK, SEL_TK)
        w = k0 // (SEL_MASK_BLOCKS * SEL_BLOCK)
        s_t = jnp.dot(k_ref[0, pl.ds(k0, SEL_TK), :], qa_ref[w], preferred_element_type=F32)
        if masked:
            kpos = k0 + jax.lax.broadcasted_iota(jnp.int32, (SEL_TK, nq), 0)
            t = start + (jax.lax.broadcasted_iota(jnp.int32, (SEL_TK, nq), 1) % Q_BLOCK)
            s_t = jnp.where(kpos <= t, s_t, MASK_NEG)
        m_old = m_ref[...]
        m_new = jnp.maximum(m_old, jnp.max(s_t, axis=0, keepdims=True))
        alpha = jnp.exp(m_old - m_new)
        p = jnp.exp(s_t - m_new)
        l_ref[...] = alpha * l_ref[...] + jnp.sum(p, axis=0, keepdims=True)
        pv = jnp.dot(vt_ref[0, :, pl.ds(k0, SEL_TK)], p.astype(BF16), preferred_element_type=F32)
        acc_ref[...] = alpha * acc_ref[...] + pv
        m_ref[...] = m_new

    n_full = start // SEL_TK

    def body(kt, c):
        tile(kt, False)
        return c

    jax.lax.fori_loop(0, n_full, body, 0)
    tile(n_full, True)
    _store_heads(o_ref, acc_ref[...] / l_ref[...])


def _sel_attn(qr_t, mb_t, k_aug, v_t):
    s = qr_t.shape[1]
    n_sel = s // SEL_BLOCK
    g4 = NSA_HPG * HEAD_DIM
    nq = NSA_HPG * Q_BLOCK
    n_half = max(n_sel // SEL_MASK_BLOCKS, 1)
    return pl.pallas_call(
        functools.partial(_sel_attn_kernel, n_sel=n_sel),
        grid=(NSA_KV_GROUPS, s // Q_BLOCK),
        in_specs=[
            pl.BlockSpec((g4, Q_BLOCK), lambda g, i: (g, i)),
            pl.BlockSpec((1, n_sel, Q_BLOCK), lambda g, i: (g, 0, i)),
            pl.BlockSpec((1, s, 2 * HEAD_DIM), lambda g, i: (g, 0, 0)),
            pl.BlockSpec((1, HEAD_DIM, s), lambda g, i: (g, 0, 0)),
        ],
        out_specs=pl.BlockSpec((g4, Q_BLOCK), lambda g, i: (g, i)),
        out_shape=jax.ShapeDtypeStruct((NSA_HEADS * HEAD_DIM, s), BF16),
        scratch_shapes=[
            pltpu.VMEM((n_half, 2 * HEAD_DIM, nq), BF16),
            pltpu.VMEM((1, nq), F32),
            pltpu.VMEM((1, nq), F32),
            pltpu.VMEM((HEAD_DIM, nq), F32),
        ],
        compiler_params=_cparams("parallel", "arbitrary"),
        name="sel_attn",
    )(qr_t, mb_t, k_aug, v_t)


def _win_gate_kernel(q_ref, k_ref, vt_ref, oc_ref, os_ref, g_ref, o_ref, *, span):
    i = pl.program_id(1)
    start = i * Q_BLOCK
    nq = NSA_HPG * Q_BLOCK
    q4 = _heads_to_lanes(q_ref[...])
    ws = pl.multiple_of(jnp.maximum(start + Q_BLOCK - span, 0), Q_BLOCK)
    s_t = jnp.dot(k_ref[0, pl.ds(ws, span), :], q4, preferred_element_type=F32)
    kpos = ws + jax.lax.broadcasted_iota(jnp.int32, (span, nq), 0)
    t = start + (jax.lax.broadcasted_iota(jnp.int32, (span, nq), 1) % Q_BLOCK)
    mask = (kpos <= t) & (kpos > t - WIN)
    s_t = jnp.where(mask, s_t, -jnp.inf)
    m = jnp.max(s_t, axis=0, keepdims=True)
    e = jnp.exp(s_t - m)
    l = jnp.sum(e, axis=0, keepdims=True)
    o_win = jnp.dot(vt_ref[0, :, pl.ds(ws, span)], e.astype(BF16), preferred_element_type=F32) / l

    gates = g_ref[0]

    def gate_row(b):
        return jnp.concatenate([gates[h * 3 + b:h * 3 + b + 1, :] for h in range(NSA_HPG)], axis=1)

    o = (gate_row(0) * _heads_to_lanes(oc_ref[...]).astype(F32)
         + gate_row(1) * _heads_to_lanes(os_ref[...]).astype(F32)
         + gate_row(2) * o_win)
    for h in range(NSA_HPG):
        o_ref[:, h * HEAD_DIM:(h + 1) * HEAD_DIM] = o[:, h * Q_BLOCK:(h + 1) * Q_BLOCK].T.astype(BF16)


def _win_gate(qr_t, k_win, v_t, oc_t, os_t, gates_t):
    s = qr_t.shape[1]
    g4 = NSA_HPG * HEAD_DIM
    span = min(WIN + Q_BLOCK, s)
    return pl.pallas_call(
        functools.partial(_win_gate_kernel, span=span),
        grid=(NSA_KV_GROUPS, s // Q_BLOCK),
        in_specs=[
            pl.BlockSpec((g4, Q_BLOCK), lambda g, i: (g, i)),
            pl.BlockSpec((1, s, HEAD_DIM), lambda g, i: (g, 0, 0)),
            pl.BlockSpec((1, HEAD_DIM, s), lambda g, i: (g, 0, 0)),
            pl.BlockSpec((g4, Q_BLOCK), lambda g, i: (g, i)),
            pl.BlockSpec((g4, Q_BLOCK), lambda g, i: (g, i)),
            pl.BlockSpec((1, GATE_ROWS, Q_BLOCK), lambda g, i: (g, 0, i)),
        ],
        out_specs=pl.BlockSpec((Q_BLOCK, g4), lambda g, i: (i, g)),
        out_shape=jax.ShapeDtypeStruct((s, NSA_HEADS * HEAD_DIM), BF16),
        compiler_params=_cparams("parallel", "arbitrary"),
        name="win_gate",
    )(qr_t, k_win, v_t, oc_t, os_t, gates_t)


def _dil_kernel(q_ref, kp_ref, kc_ref, vp_ref, vc_ref, o_ref, lse_ref, *, tiles_per_seq, nkeys):
    i = pl.program_id(0)
    first = (i % tiles_per_seq) == 0
    qq = jax.lax.broadcasted_iota(jnp.int32, (Q_BLOCK, 2 * Q_BLOCK), 0)
    kk = jax.lax.broadcasted_iota(jnp.int32, (Q_BLOCK, 2 * Q_BLOCK), 1) - Q_BLOCK
    diff = qq - kk
    mask = (diff >= 0) & (diff < nkeys) & ((kk >= 0) | jnp.logical_not(first))
    for h in range(DIL_HPG):
        k2 = jnp.concatenate([kp_ref[h], kc_ref[h]], axis=0)
        v2 = jnp.concatenate([vp_ref[h], vc_ref[h]], axis=0)
        s = jax.lax.dot_general(q_ref[h], k2, (((1,), (1,)), ((), ())), preferred_element_type=F32)
        s = jnp.where(mask, s, -jnp.inf)
        m = jnp.max(s, axis=1, keepdims=True)
        e = jnp.exp(s - m)
        l = jnp.sum(e, axis=1, keepdims=True)
        o = jnp.dot(e.astype(BF16), v2, preferred_element_type=F32) / l
        o_ref[0, :, h * HEAD_DIM:(h + 1) * HEAD_DIM] = o
        lse_ref[0, :, h * HEAD_DIM:(h + 1) * HEAD_DIM] = jnp.broadcast_to(m + jnp.log(l), (Q_BLOCK, HEAD_DIM))


def _dilated_group(q, k, v, win, dil):
    s = q.shape[1]
    tps = s // (dil * Q_BLOCK)
    nkeys = win // dil + 1
    assert nkeys <= Q_BLOCK + 1
    blk = (DIL_HPG, Q_BLOCK, HEAD_DIM)
    cur = lambda i: (0, i, 0)
    prev = lambda i: (0, jnp.maximum(i - 1, 0), 0)
    w = DIL_HPG * HEAD_DIM
    out_spec = pl.BlockSpec((1, Q_BLOCK, w), lambda i: (i % tps, 0, i // tps))
    out_shape = jax.ShapeDtypeStruct((tps, Q_BLOCK, dil * w), F32)
    o, lse = pl.pallas_call(
        functools.partial(_dil_kernel, tiles_per_seq=tps, nkeys=nkeys),
        grid=(s // Q_BLOCK,),
        in_specs=[
            pl.BlockSpec(blk, cur),
            pl.BlockSpec(blk, prev), pl.BlockSpec(blk, cur),
            pl.BlockSpec(blk, prev), pl.BlockSpec(blk, cur),
        ],
        out_specs=[out_spec, out_spec],
        out_shape=[out_shape, out_shape],
        compiler_params=_cparams("parallel"),
        name="dilated",
    )(q, k, k, v, v)
    return o.reshape(s, w), lse.reshape(s, w)


def _merge_kernel(oa_ref, o0_ref, o1_ref, o2_ref, l0_ref, l1_ref, l2_ref, ga_ref, gb_ref, h_ref,
                  wa_ref, wb_ref, wo_ref, g_ref, b_ref, o_ref, ob_ref):
    l0, l1, l2 = l0_ref[...], l1_ref[...], l2_ref[...]
    mx = jnp.maximum(jnp.maximum(l0, l1), l2)
    e0, e1, e2 = jnp.exp(l0 - mx), jnp.exp(l1 - mx), jnp.exp(l2 - mx)
    o_b = (e0 * o0_ref[...] + e1 * o1_ref[...] + e2 * o2_ref[...]) / (e0 + e1 + e2)
    y_a = jnp.dot(oa_ref[...], wa_ref[...], preferred_element_type=F32)
    y_b = jnp.dot(o_b.astype(BF16), wb_ref[...], preferred_element_type=F32)
    merged = ga_ref[...].astype(F32) * y_a + gb_ref[...].astype(F32) * y_b
    mix = jnp.dot(merged.astype(BF16), wo_ref[...], preferred_element_type=F32)
    out = _layer_norm(DEEPNORM_ALPHA * h_ref[...] + mix, g_ref[...], b_ref[...])
    o_ref[...] = out
    ob_ref[...] = out.astype(BF16)


def _merge(o_a, dil_o, dil_lse, gab, h1, wa, wb, wo, g, b, *, tm=256):
    s = h1.shape[0]
    tm = min(tm, s)
    wb_cols = DIL_HPG * HEAD_DIM
    row = lambda w: pl.BlockSpec((tm, w), lambda i: (i, 0))
    full = lambda a: pl.BlockSpec(a.shape, lambda i: (0, 0))
    return pl.pallas_call(
        _merge_kernel,
        grid=(s // tm,),
        in_specs=[
            row(NSA_HEADS * HEAD_DIM),
            row(wb_cols), row(wb_cols), row(wb_cols),
            row(wb_cols), row(wb_cols), row(wb_cols),
            pl.BlockSpec((tm, D_MODEL), lambda i: (i, 0)),
            pl.BlockSpec((tm, D_MODEL), lambda i: (i, 1)),
            row(D_MODEL),
            full(wa), full(wb), full(wo), full(g), full(b),
        ],
        out_specs=[row(D_MODEL), row(D_MODEL)],
        out_shape=[jax.ShapeDtypeStruct((s, D_MODEL), F32), jax.ShapeDtypeStruct((s, D_MODEL), BF16)],
        compiler_params=_cparams("parallel"),
        name="merge_out",
    )(o_a, *dil_o, *dil_lse, gab, gab, h1, wa, wb, wo, g, b)


def _cmp_to_sel_t(s):
    nrow = s // CMP_STRIDE
    n_sel = s // SEL_BLOCK
    c0 = jnp.arange(nrow) * CMP_STRIDE
    s0 = jnp.arange(n_sel) * SEL_BLOCK
    overlap = jnp.clip(jnp.minimum(c0[None, :] + CMP_BLOCK, s0[:, None] + SEL_BLOCK)
                       - jnp.maximum(c0[None, :], s0[:, None]), 0)
    return (overlap.astype(F32) / CMP_STRIDE).astype(BF16)


def _mixer_branches(h1b, w_in, ck_pos, ck_w1, ck_w2, cv_pos, cv_w1, cv_w2):
    s = h1b.shape[0]
    bf = lambda a: a.astype(BF16)
    w_in = bf(w_in)
    sizes = (NSA_Q_W,) + (NSA_KV_W,) * 6 + (NSA_GATE_W,) + (DIL_W,) * 3 + (D_MODEL, D_MODEL)
    offs = [0]
    for z in sizes:
        offs.append(offs[-1] + z)
    col = lambda n: w_in[:, offs[n]:offs[n + 1]]
    w_qa, w_kc, w_vc, w_ksl, w_vsl, w_kwn, w_vwn, w_gn, w_qb, w_kb, w_vb = (col(n) for n in range(11))
    w_gab = w_in[:, offs[11]:offs[13]]

    cos, sin = _rope_tables(s)
    qp_t, qr_t = _proj_t(h1b, w_qa.T, mode="qa", tables_t=(cos.T, sin.T))
    cmp_tok = _proj_nat(h1b, jnp.concatenate([w_kc, w_vc], axis=1))
    k_aug = _proj_nat(h1b, w_ksl, rope=True, aug=True, tables=(cos, sin))
    k_win = _proj_nat(h1b, w_kwn, rope=True, tables=(cos, sin))
    v_t = _proj_t(h1b, jnp.concatenate([w_vsl, w_vwn], axis=1).T, mode="plain")
    v_t = v_t.reshape(4, HEAD_DIM, s)
    w_gn_t = w_gn.T.reshape(NSA_KV_GROUPS, NSA_HPG * 3, D_MODEL)
    w_gn_t = jnp.pad(w_gn_t, ((0, 0), (0, GATE_ROWS - NSA_HPG * 3), (0, 0))).reshape(-1, D_MODEL)
    gates_t = _proj_t(h1b, w_gn_t, mode="sigmoid", out_dtype=F32).reshape(NSA_KV_GROUPS, GATE_ROWS, s)
    gab = _proj_sig(h1b, w_gab)

    pos = jnp.stack([ck_pos, cv_pos]).reshape(2, 1, CMP_BLOCK * HEAD_DIM)
    w1 = bf(jnp.stack([ck_w1, cv_w1])).reshape(2, 2, CMP_STRIDE * HEAD_DIM, CMP_HIDDEN)
    w2 = bf(jnp.stack([ck_w2, cv_w2]))
    cmp_nat, cmp_t = _compress(cmp_tok, pos, w1, w2)
    oc_t, mb_t = _cmp_topk(qp_t, cmp_nat, cmp_t, _cmp_to_sel_t(s))
    os_t = _sel_attn(qr_t, mb_t, k_aug, v_t[:NSA_KV_GROUPS])
    o_a = _win_gate(qr_t, k_win, v_t[NSA_KV_GROUPS:], oc_t, os_t, gates_t)

    dil_o, dil_lse = [], []
    gw = DIL_HPG * HEAD_DIM
    for gi, (win, dil) in enumerate(DIL_PATTERNS):
        sl = slice(gi * gw, (gi + 1) * gw)
        q = _proj_nat(h1b, w_qb[:, sl], rope=True, scale=SCALE, dil=dil, tables=(cos, sin))
        k = _proj_nat(h1b, w_kb[:, sl], rope=True, dil=dil, tables=(cos, sin))
        v = _proj_nat(h1b, w_vb[:, sl], dil=dil)
        o, lse = _dilated_group(q, k, v, win, dil)
        dil_o.append(o)
        dil_lse.append(lse)
    return o_a, dil_o, dil_lse, gab


def _layer(h, wts):
    (ffn1_wg, ffn1_wu, ffn1_wd, ln1_g, ln1_b, w_in, ck_pos, ck_w1, ck_w2, cv_pos, cv_w1, cv_w2,
     w_a, w_b, w_o, lnm_g, lnm_b, ffn2_wg, ffn2_wu, ffn2_wd, ln2_g, ln2_b) = wts
    bf = lambda a: a.astype(BF16)
    vec = lambda a: a.reshape(1, D_MODEL)

    h1, h1b = _ffn_ln(h, bf(ffn1_wg), bf(ffn1_wu), bf(ffn1_wd), vec(ln1_g), vec(ln1_b))
    o_a, dil_o, dil_lse, gab = _mixer_branches(h1b, w_in, ck_pos, ck_w1, ck_w2, cv_pos, cv_w1, cv_w2)
    h2, _ = _merge(o_a, dil_o, dil_lse, gab, h1, bf(w_a), bf(w_b), bf(w_o), vec(lnm_g), vec(lnm_b))
    h3, _ = _ffn_ln(h2, bf(ffn2_wg), bf(ffn2_wu), bf(ffn2_wd), vec(ln2_g), vec(ln2_b))
    return h3


def kernel(x, ffn1_w_gate, ffn1_w_up, ffn1_w_down, ln_ffn1_g, ln_ffn1_b, w_in, cmp_k_pos, cmp_k_w1, cmp_k_w2, cmp_v_pos, cmp_v_w1, cmp_v_w2, w_branch_a, w_branch_b, w_o, ln_mix_g, ln_mix_b, ffn2_w_gate, ffn2_w_up, ffn2_w_down, ln_ffn2_g, ln_ffn2_b):
    b, s, _ = x.shape
    stacked = (ffn1_w_gate, ffn1_w_up, ffn1_w_down, ln_ffn1_g, ln_ffn1_b, w_in, cmp_k_pos, cmp_k_w1,
               cmp_k_w2, cmp_v_pos, cmp_v_w1, cmp_v_w2, w_branch_a, w_branch_b, w_o, ln_mix_g, ln_mix_b,
               ffn2_w_gate, ffn2_w_up, ffn2_w_down, ln_ffn2_g, ln_ffn2_b)
    outs = []
    for bi in range(b):
        h = x[bi]
        for l in range(DEPTH):
            h = _layer(h, tuple(w[l] for w in stacked))
        outs.append(h)
    return jnp.stack(outs)
```
